```python
import math
import jax, jax.numpy as jnp
from jax import lax
import numpy as np

D_MODEL = 2048
BATCH = 8
SEQ = 4096
DEPTH = 4

ATTN_HEADS = 8
QK_DIM = 64
V_DIM = 2 * QK_DIM
QK_WIDTH = ATTN_HEADS * 2 * QK_DIM
ATTN_WIDTH = ATTN_HEADS * V_DIM
Q_BLOCK = 128
ROPE_THETA = 10000.0
LRU_WIDTH = D_MODEL - ATTN_WIDTH
LRU_BLOCKS = 8
LRU_BLOCK = LRU_WIDTH // LRU_BLOCKS
LRU_CONV = 4
LRU_C = 8.0
D_FF = 3 * D_MODEL
FFN_CONV = 3
IN_WIDTH = 2 * QK_WIDTH + ATTN_WIDTH + 2 * LRU_WIDTH
EPS = 1e-6

kernel_name = "hymba_diffattn_rglru_convffn_trunk"


def lambda_init_fn(layer_idx):
    return 0.8 - 0.6 * math.exp(-0.3 * layer_idx)


def rmsnorm(x, gain):
    x32 = x.astype(jnp.float32)
    y = x32 * lax.rsqrt(jnp.mean(x32 * x32, axis=-1, keepdims=True) + EPS)
    return (y * gain.astype(jnp.float32)).astype(x.dtype)


def causal_dwconv(x, w, b):
    K = w.shape[0]
    S = x.shape[1]
    xp = jnp.pad(x, ((0, 0), (K - 1, 0), (0, 0)))
    y = b + xp[:, 0:S] * w[0]
    for j in range(1, K):
        y = y + xp[:, j:j + S] * w[j]
    return y


def rope_tables(positions):
    inv_freq = 1.0 / (ROPE_THETA ** (jnp.arange(0, QK_DIM, 2, dtype=jnp.float32) / QK_DIM))
    ang = positions.astype(jnp.float32)[..., None] * inv_freq
    return jnp.cos(ang), jnp.sin(ang)


def apply_rope(t, cos, sin):
    t32 = t.astype(jnp.float32)
    c = cos[:, :, None, None, :]
    s = sin[:, :, None, None, :]
    t1, t2 = jnp.split(t32, 2, axis=-1)
    return jnp.concatenate([t1 * c - t2 * s, t2 * c + t1 * s], axis=-1).astype(t.dtype)


def diff_attention(q, k, v, lam, subln_g, lambda_init):
    B, S = q.shape[0], q.shape[1]
    qh = q.transpose(0, 2, 3, 1, 4)
    kh = k.transpose(0, 2, 3, 1, 4)
    vh = v.transpose(0, 2, 1, 3)
    scale = QK_DIM ** -0.5
    outs = []
    for qb in range(S // Q_BLOCK):
        qs, qe = qb * Q_BLOCK, (qb + 1) * Q_BLOCK
        s = jnp.einsum('bhcqd,bhckd->bhcqk', qh[:, :, :, qs:qe], kh[:, :, :, :qe]).astype(jnp.float32) * scale
        mask = jnp.arange(qs, qe)[:, None] >= jnp.arange(qe)[None, :]
        s = jnp.where(mask, s, -jnp.inf)
        p = jax.nn.softmax(s, axis=-1)
        a = p[:, :, 0] - lam * p[:, :, 1]
        outs.append(jnp.einsum('bhqk,bhkd->bhqd', a, vh[:, :, :qe].astype(jnp.float32)))
    o = jnp.concatenate(outs, axis=2)
    o = rmsnorm(o, subln_g) * (1.0 - lambda_init)
    return o.transpose(0, 2, 1, 3).reshape(B, S, ATTN_WIDTH).astype(v.dtype)


def rglru_branch(xb, gb, conv_w, conv_b, ga_w, ga_b, gx_w, gx_b, lru_lambda, lru_norm):
    B, S = xb.shape[0], xb.shape[1]
    xc = causal_dwconv(xb, conv_w, conv_b)
    xh = xc.reshape(B, S, LRU_BLOCKS, LRU_BLOCK)
    r = jax.nn.sigmoid(jnp.einsum('bshi,hij->bshj', xh, ga_w) + ga_b).reshape(B, S, LRU_WIDTH)
    i = jax.nn.sigmoid(jnp.einsum('bshi,hij->bshj', xh, gx_w) + gx_b).reshape(B, S, LRU_WIDTH)
    r32 = r.astype(jnp.float32)
    log_a = -LRU_C * r32 * jax.nn.softplus(-lru_lambda.astype(jnp.float32))
    a = jnp.exp(log_a)
    mult = jnp.sqrt(-jnp.expm1(2.0 * log_a))
    bterm = mult * (i.astype(jnp.float32) * xc.astype(jnp.float32))

    def combine(left, right):
        a1, b1 = left
        a2, b2 = right
        return a1 * a2, a2 * b1 + b2

    _, h = lax.associative_scan(combine, (a, bterm), axis=1)
    y = h.astype(xb.dtype) * jax.nn.gelu(gb)
    return rmsnorm(y, lru_norm)


def conv_glu_mlp(h, w_up, cw, cb, w_down):
    u = causal_dwconv(h @ w_up, cw, cb)
    g, val = jnp.split(u, 2, axis=-1)
    return (jax.nn.gelu(g) * val) @ w_down


def setup_inputs(seed: int = 0) -> dict:
    key = jax.random.key(seed)
    ks = jax.random.split(key, 24)
    f32 = jnp.float32
    n = lambda k, shp, s: jax.random.normal(k, shp, f32) * s
    x = jax.random.normal(ks[0], (BATCH, SEQ, D_MODEL), f32)
    offs = jax.random.randint(ks[1], (BATCH, 1), 0, 1024, dtype=jnp.int32)
    positions = (offs + jnp.arange(SEQ, dtype=jnp.int32)[None, :]).astype(jnp.int32)
    u = jax.random.uniform(ks[2], (DEPTH, LRU_WIDTH), f32, minval=0.9, maxval=0.999)
    a0 = u ** (1.0 / LRU_C)
    lru_lambda = jnp.log(a0) - jnp.log1p(-a0)
    return {
        "x": x,
        "positions": positions,
        "attn_norm": 1.0 + n(ks[3], (DEPTH, D_MODEL), 0.02),
        "w_in": n(ks[4], (DEPTH, D_MODEL, IN_WIDTH), D_MODEL ** -0.5),
        "lambda_q1": n(ks[5], (DEPTH, QK_DIM), 0.1),
        "lambda_k1": n(ks[6], (DEPTH, QK_DIM), 0.1),
        "lambda_q2": n(ks[7], (DEPTH, QK_DIM), 0.1),
        "lambda_k2": n(ks[8], (DEPTH, QK_DIM), 0.1),
        "subln": 1.0 + n(ks[9], (DEPTH, V_DIM), 0.02),
        "lru_conv_w": n(ks[10], (DEPTH, LRU_CONV, LRU_WIDTH), LRU_CONV ** -0.5),
        "lru_conv_b": n(ks[11], (DEPTH, LRU_WIDTH), 0.01),
        "gate_a_w": n(ks[12], (DEPTH, LRU_BLOCKS, LRU_BLOCK, LRU_BLOCK), LRU_BLOCK ** -0.5),
        "gate_a_b": n(ks[13], (DEPTH, LRU_BLOCKS, LRU_BLOCK), 0.01),
        "gate_x_w": n(ks[14], (DEPTH, LRU_BLOCKS, LRU_BLOCK, LRU_BLOCK), LRU_BLOCK ** -0.5),
        "gate_x_b": n(ks[15], (DEPTH, LRU_BLOCKS, LRU_BLOCK), 0.01),
        "lru_lambda": lru_lambda,
        "lru_norm": 1.0 + n(ks[16], (DEPTH, LRU_WIDTH), 0.02),
        "w_out": n(ks[17], (DEPTH, D_MODEL, D_MODEL), D_MODEL ** -0.5),
        "mlp_norm": 1.0 + n(ks[18], (DEPTH, D_MODEL), 0.02),
        "w_up": n(ks[19], (DEPTH, D_MODEL, 2 * D_FF), D_MODEL ** -0.5),
        "ffn_conv_w": n(ks[20], (DEPTH, FFN_CONV, 2 * D_FF), FFN_CONV ** -0.5),
        "ffn_conv_b": n(ks[21], (DEPTH, 2 * D_FF), 0.01),
        "w_down": n(ks[22], (DEPTH, D_FF, D_MODEL), D_FF ** -0.5),
        "final_norm": 1.0 + n(ks[23], (D_MODEL,), 0.02),
    }


def reference(x, positions, attn_norm, w_in, lambda_q1, lambda_k1, lambda_q2, lambda_k2,
              subln, lru_conv_w, lru_conv_b, gate_a_w, gate_a_b, gate_x_w, gate_x_b,
              lru_lambda, lru_norm, w_out, mlp_norm, w_up, ffn_conv_w, ffn_conv_b,
              w_down, final_norm):
    B, S = x.shape[0], x.shape[1]
    cos, sin = rope_tables(positions)
    splits = [QK_WIDTH, 2 * QK_WIDTH, 2 * QK_WIDTH + ATTN_WIDTH, 2 * QK_WIDTH + ATTN_WIDTH + LRU_WIDTH]
    for l in range(DEPTH):
        lam_init = lambda_init_fn(l)
        h = rmsnorm(x, attn_norm[l])
        proj = h @ w_in[l]
        q, k, v, xb, gb = jnp.split(proj, splits, axis=-1)
        q = apply_rope(q.reshape(B, S, ATTN_HEADS, 2, QK_DIM), cos, sin)
        k = apply_rope(k.reshape(B, S, ATTN_HEADS, 2, QK_DIM), cos, sin)
        v = v.reshape(B, S, ATTN_HEADS, V_DIM)
        lam = (jnp.exp(jnp.sum(lambda_q1[l].astype(jnp.float32) * lambda_k1[l].astype(jnp.float32)))
               - jnp.exp(jnp.sum(lambda_q2[l].astype(jnp.float32) * lambda_k2[l].astype(jnp.float32)))
               + lam_init)
        attn_out = diff_attention(q, k, v, lam, subln[l], lam_init)
        lru_out = rglru_branch(xb, gb, lru_conv_w[l], lru_conv_b[l], gate_a_w[l], gate_a_b[l],
                               gate_x_w[l], gate_x_b[l], lru_lambda[l], lru_norm[l])
        mixed = jnp.concatenate([attn_out.astype(x.dtype), lru_out.astype(x.dtype)], axis=-1)
        x = x + mixed @ w_out[l]
        h = rmsnorm(x, mlp_norm[l])
        x = x + conv_glu_mlp(h, w_up[l], ffn_conv_w[l], ffn_conv_b[l], w_down[l])
    return rmsnorm(x, final_norm)
```

```python
import functools
import math

import jax
import jax.numpy as jnp
from jax import lax
from jax.experimental import pallas as pl
from jax.experimental.pallas import tpu as pltpu

EPS = 1e-6
ROPE_THETA = 10000.0
LRU_C = 8.0
LANES = 128
SUBLANES = 8
VMEM_LIMIT = 56 * 1024 * 1024

F32 = jnp.float32
BF16 = jnp.bfloat16


def _params(n_axes):
    return pltpu.CompilerParams(dimension_semantics=("arbitrary",) * n_axes,
                                vmem_limit_bytes=VMEM_LIMIT)


def _rms(x, gain):
    ms = jnp.mean(x * x, axis=-1, keepdims=True)
    return x * lax.rsqrt(ms + EPS) * gain


def _rope_table_kernel(pos_ref, inv_ref, sign_ref, cos_ref, sin_ref):
    ang = pos_ref[...] * inv_ref[...]
    cos_ref[...] = jnp.cos(ang)
    sin_ref[...] = jnp.sin(ang) * sign_ref[...]


def _rope_tables(pos_f, dk, tm):
    n = pos_f.shape[0]
    half = dk // 2
    inv_freq = 1.0 / (ROPE_THETA ** (jnp.arange(0, dk, 2, dtype=F32) / dk))
    reps = LANES // half
    inv = jnp.tile(inv_freq, reps).reshape(1, LANES)
    sign = jnp.tile(jnp.concatenate([-jnp.ones((half,), F32), jnp.ones((half,), F32)]), reps // 2).reshape(1, LANES)
    return pl.pallas_call(
        _rope_table_kernel,
        grid=(n // tm,),
        in_specs=[pl.BlockSpec((tm, 1), lambda i: (i, 0)),
                  pl.BlockSpec((1, LANES), lambda i: (0, 0)),
                  pl.BlockSpec((1, LANES), lambda i: (0, 0))],
        out_specs=[pl.BlockSpec((tm, LANES), lambda i: (i, 0)),
                   pl.BlockSpec((tm, LANES), lambda i: (i, 0))],
        out_shape=[jax.ShapeDtypeStruct((n, LANES), F32)] * 2,
        compiler_params=_params(1),
        name="rope_tables",
    )(pos_f, inv, sign)


def _in_proj_kernel(x_ref, g_ref, w_ref, cos_ref, sin_ref,
                    q_ref, k_ref, v_ref, xb_ref, gb_ref, h_ref, *, heads, half):
    j = pl.program_id(1)

    @pl.when(j == 0)
    def _():
        h_ref[...] = _rms(x_ref[...], g_ref[...]).astype(BF16)

    def proj():
        return jnp.dot(h_ref[...], w_ref[...], preferred_element_type=F32)

    def rope_store(o_ref):
        acc = proj()
        c = cos_ref[...]
        s = sin_ref[...]
        lane = lax.broadcasted_iota(jnp.int32, c.shape, 1)
        first = (lane % (2 * half)) < half
        for hd in range(heads):
            t = acc[:, hd * LANES:(hd + 1) * LANES]
            partner = jnp.where(first, pltpu.roll(t, LANES - half, 1), pltpu.roll(t, half, 1))
            o_ref[:, hd * LANES:(hd + 1) * LANES] = (t * c + partner * s).astype(o_ref.dtype)

    @pl.when(j == 0)
    def _():
        rope_store(q_ref)

    @pl.when(j == 1)
    def _():
        rope_store(k_ref)

    @pl.when(j == 2)
    def _():
        v_ref[...] = proj().astype(v_ref.dtype)

    @pl.when(j == 3)
    def _():
        xb_ref[...] = proj()

    @pl.when(j == 4)
    def _():
        gb_ref[...] = proj()


def _in_proj(x2, gain, w_in, cos_t, sin_t, *, width, heads, half, tm):
    n, d = x2.shape
    out_spec = pl.BlockSpec((tm, width), lambda i, j: (i, 0))
    return pl.pallas_call(
        functools.partial(_in_proj_kernel, heads=heads, half=half),
        grid=(n // tm, 5),
        in_specs=[pl.BlockSpec((tm, d), lambda i, j: (i, 0)),
                  pl.BlockSpec((1, d), lambda i, j: (0, 0)),
                  pl.BlockSpec((d, width), lambda i, j: (0, j)),
                  pl.BlockSpec((tm, LANES), lambda i, j: (i, 0)),
                  pl.BlockSpec((tm, LANES), lambda i, j: (i, 0))],
        out_specs=[out_spec] * 5,
        out_shape=[jax.ShapeDtypeStruct((n, width), BF16)] * 3 + [jax.ShapeDtypeStruct((n, width), F32)] * 2,
        scratch_shapes=[pltpu.VMEM((tm, d), BF16)],
        compiler_params=_params(2),
        name="in_proj",
    )(x2, gain, w_in, cos_t, sin_t)


def _attn_kernel(q_ref, k_ref, v_ref, lq1_ref, lk1_ref, lq2_ref, lk2_ref, li_ref, sg_ref,
                 o_ref, q2_ref, m_ref, l_ref, acc_ref, *, tq, dk):
    qi = pl.program_id(2)
    scale = dk ** -0.5
    q = q_ref[...] * jnp.asarray(scale, q_ref.dtype)
    lane = lax.broadcasted_iota(jnp.int32, q.shape, 1)
    lo = lane < dk
    zero = jnp.zeros_like(q)
    q2_ref[0:tq, :] = jnp.where(lo, q, zero)
    q2_ref[tq:2 * tq, :] = jnp.where(lo, zero, q)
    m_ref[...] = jnp.full(m_ref.shape, -jnp.inf, F32)
    l_ref[...] = jnp.zeros(l_ref.shape, F32)
    acc_ref[...] = jnp.zeros(acc_ref.shape, F32)

    def scores(ki):
        kt = k_ref[pl.ds(pl.multiple_of(ki * tq, tq), tq), :]
        return lax.dot_general(q2_ref[...], kt, (((1,), (1,)), ((), ())), preferred_element_type=F32)

    def update(s, ki):
        vt = v_ref[pl.ds(pl.multiple_of(ki * tq, tq), tq), :]
        m_prev = m_ref[...]
        m_new = jnp.maximum(m_prev, jnp.max(s, axis=1, keepdims=True))
        alpha = jnp.exp(m_prev - m_new)
        p = jnp.exp(s - m_new)
        l_ref[...] = alpha * l_ref[...] + jnp.sum(p, axis=1, keepdims=True)
        acc_ref[...] = alpha * acc_ref[...] + jnp.dot(p.astype(BF16), vt, preferred_element_type=F32)
        m_ref[...] = m_new

    def body(ki, carry):
        update(scores(ki), ki)
        return carry

    lax.fori_loop(0, qi, body, 0)

    s = scores(qi)
    row = lax.broadcasted_iota(jnp.int32, s.shape, 0)
    col = lax.broadcasted_iota(jnp.int32, s.shape, 1)
    row = jnp.where(row >= tq, row - tq, row)
    update(jnp.where(row >= col, s, -jnp.inf), qi)

    lam_init = li_ref[...]
    lam = (jnp.exp(jnp.sum(lq1_ref[...] * lk1_ref[...], axis=1, keepdims=True))
           - jnp.exp(jnp.sum(lq2_ref[...] * lk2_ref[...], axis=1, keepdims=True)) + lam_init)
    o_all = acc_ref[...] / l_ref[...]
    o = o_all[0:tq, :] - lam * o_all[tq:2 * tq, :]
    o_ref[...] = (_rms(o, sg_ref[...]) * (1.0 - lam_init)).astype(o_ref.dtype)


def _attention(q, k, v, lq1, lk1, lq2, lk2, lam_init, subln, *, batch, seq, heads, dk, tq):
    n, width = q.shape
    nq = seq // tq
    vec = lambda c: pl.BlockSpec((1, c), lambda b, h, i: (0, 0))
    return pl.pallas_call(
        functools.partial(_attn_kernel, tq=tq, dk=dk),
        grid=(batch, heads, nq),
        in_specs=[pl.BlockSpec((tq, LANES), lambda b, h, i: (b * nq + i, h)),
                  pl.BlockSpec((seq, LANES), lambda b, h, i: (b, h)),
                  pl.BlockSpec((seq, LANES), lambda b, h, i: (b, h)),
                  vec(dk), vec(dk), vec(dk), vec(dk), vec(1), vec(LANES)],
        out_specs=pl.BlockSpec((tq, LANES), lambda b, h, i: (b * nq + i, h)),
        out_shape=jax.ShapeDtypeStruct((n, width), BF16),
        scratch_shapes=[pltpu.VMEM((2 * tq, LANES), BF16),
                        pltpu.VMEM((2 * tq, 1), F32),
                        pltpu.VMEM((2 * tq, 1), F32),
                        pltpu.VMEM((2 * tq, LANES), F32)],
        compiler_params=_params(3),
        name="diff_attention",
    )(q, k, v, lq1, lk1, lq2, lk2, lam_init, subln)


def _linear_scan(a, b):
    t = a.shape[0]
    row = lax.broadcasted_iota(jnp.int32, a.shape, 0)
    d = 1
    while d < min(t, SUBLANES):
        keep = row >= d
        a_s = pltpu.roll(a, d, 0)
        b_s = pltpu.roll(b, d, 0)
        b = jnp.where(keep, a * b_s + b, b)
        a = jnp.where(keep, a * a_s, a)
        d *= 2
    while d < t:
        b = jnp.concatenate([b[:d], a[d:] * b[:t - d] + b[d:]], axis=0)
        a = jnp.concatenate([a[:d], a[d:] * a[:t - d]], axis=0)
        d *= 2
    return a, b


def _rglru_kernel(xb_ref, gb_ref, cw_ref, cb_ref, wg_ref, bga_ref, bgx_ref, lam_ref, gn_ref,
                  o_ref, xs_ref, h_ref, y_ref, *, tc, blocks, taps):
    t = pl.program_id(1)
    pad = SUBLANES

    @pl.when(t == 0)
    def _():
        xs_ref[0:pad, :] = jnp.zeros((pad, xs_ref.shape[1]), F32)
        h_ref[...] = jnp.zeros(h_ref.shape, F32)

    xs_ref[pad:pad + tc, :] = xb_ref[...]
    ssq = jnp.zeros((tc, 1), F32)
    for hb in range(blocks):
        cs = slice(hb * LANES, (hb + 1) * LANES)
        xc = cb_ref[:, cs] + xs_ref[pad - (taps - 1):pad - (taps - 1) + tc, cs] * cw_ref[0:1, cs]
        for kk in range(1, taps):
            off = pad - (taps - 1) + kk
            xc = xc + xs_ref[off:off + tc, cs] * cw_ref[kk:kk + 1, cs]
        gates = jnp.dot(xc.astype(BF16), wg_ref[hb], preferred_element_type=F32)
        r = jax.nn.sigmoid(gates[:, 0:LANES] + bga_ref[:, cs])
        i = jax.nn.sigmoid(gates[:, LANES:2 * LANES] + bgx_ref[:, cs])
        log_a = -LRU_C * r * jax.nn.softplus(-lam_ref[:, cs])
        a = jnp.exp(log_a)
        mult = jnp.sqrt(jnp.tanh(-log_a) * (1.0 + a * a))
        bterm = mult * (i * xc)
        a_cum, h_loc = _linear_scan(a, bterm)
        h = a_cum * h_ref[0:1, cs] + h_loc
        h_ref[0:1, cs] = h[tc - 1:tc, :]
        y = h * jax.nn.gelu(gb_ref[:, cs])
        y_ref[:, cs] = y
        ssq = ssq + jnp.sum(y * y, axis=1, keepdims=True)

    xs_ref[0:pad, :] = xs_ref[tc:tc + pad, :]
    width = y_ref.shape[1]
    inv = lax.rsqrt(ssq / width + EPS)
    o_ref[...] = (y_ref[...] * inv * gn_ref[...]).astype(o_ref.dtype)


def _rglru(xb, gb, conv_w, conv_b, wg, bga, bgx, lam, gnorm, *, batch, seq, tc):
    n, width = xb.shape
    nt = seq // tc
    blocks = width // LANES
    taps = conv_w.shape[0]
    row = lambda b, t: (b * nt + t, 0)
    fixed2 = lambda b, t: (0, 0)
    vec = pl.BlockSpec((1, width), fixed2)
    return pl.pallas_call(
        functools.partial(_rglru_kernel, tc=tc, blocks=blocks, taps=taps),
        grid=(batch, nt),
        in_specs=[pl.BlockSpec((tc, width), row),
                  pl.BlockSpec((tc, width), row),
                  pl.BlockSpec((taps, width), fixed2),
                  vec,
                  pl.BlockSpec((blocks, LANES, 2 * LANES), lambda b, t: (0, 0, 0)),
                  vec, vec, vec, vec],
        out_specs=pl.BlockSpec((tc, width), row),
        out_shape=jax.ShapeDtypeStruct((n, width), BF16),
        scratch_shapes=[pltpu.VMEM((tc + SUBLANES, width), F32),
                        pltpu.VMEM((SUBLANES, width), F32),
                        pltpu.VMEM((tc, width), F32)],
        compiler_params=_params(2),
        name="rglru",
    )(xb, gb, conv_w, conv_b, wg, bga, bgx, lam, gnorm)


def _out_proj_kernel(a_ref, r_ref, wa_ref, wr_ref, x_ref, o_ref):
    o_ref[...] = (x_ref[...]
                  + jnp.dot(a_ref[...], wa_ref[...], preferred_element_type=F32)
                  + jnp.dot(r_ref[...], wr_ref[...], preferred_element_type=F32))


def _out_proj(attn, lru, w_out, x2, *, tm, tn):
    n, d = x2.shape
    wa = attn.shape[1]
    wr = lru.shape[1]
    assert wa == wr, "the two mixer groups must be equally wide"
    return pl.pallas_call(
        _out_proj_kernel,
        grid=(n // tm, d // tn),
        in_specs=[pl.BlockSpec((tm, wa), lambda i, j: (i, 0)),
                  pl.BlockSpec((tm, wr), lambda i, j: (i, 0)),
                  pl.BlockSpec((wa, tn), lambda i, j: (0, j)),
                  pl.BlockSpec((wr, tn), lambda i, j: (1, j)),
                  pl.BlockSpec((tm, tn), lambda i, j: (i, j))],
        out_specs=pl.BlockSpec((tm, tn), lambda i, j: (i, j)),
        out_shape=jax.ShapeDtypeStruct((n, d), F32),
        compiler_params=_params(2),
        name="out_proj",
    )(attn, lru, w_out, w_out, x2)


def _ffn_kernel(x_ref, g_ref, wg_ref, wv_ref, cwg_ref, cwv_ref, cbg_ref, cbv_ref, wd_ref,
                o_ref, h_ref, tg_ref, tv_ref, bg_ref, bv_ref, *, tm, taps):
    t = pl.program_id(1)
    j = pl.program_id(2)
    pad = SUBLANES

    @pl.when(j == 0)
    def _():
        h_ref[...] = _rms(x_ref[...], g_ref[...]).astype(BF16)

    def conv(w_ref, cw_ref, cb_ref, tail_ref, buf_ref):
        u = jnp.dot(h_ref[...], w_ref[...], preferred_element_type=F32)

        @pl.when(t == 0)
        def _():
            buf_ref[0:pad, :] = jnp.zeros((pad, buf_ref.shape[1]), F32)

        @pl.when(t > 0)
        def _():
            buf_ref[0:pad, :] = tail_ref[j]

        buf_ref[pad:pad + tm, :] = u
        tail_ref[j] = u[tm - pad:tm, :]
        y = cb_ref[...] + buf_ref[pad - (taps - 1):pad - (taps - 1) + tm, :] * cw_ref[0:1, :]
        for kk in range(1, taps - 1):
            off = pad - (taps - 1) + kk
            y = y + buf_ref[off:off + tm, :] * cw_ref[kk:kk + 1, :]
        return y + u * cw_ref[taps - 1:taps, :]

    yg = conv(wg_ref, cwg_ref, cbg_ref, tg_ref, bg_ref)
    yv = conv(wv_ref, cwv_ref, cbv_ref, tv_ref, bv_ref)
    act = (jax.nn.gelu(yg) * yv).astype(BF16)
    contrib = jnp.dot(act, wd_ref[...], preferred_element_type=F32)

    @pl.when(j == 0)
    def _():
        o_ref[...] = x_ref[...] + contrib

    @pl.when(j > 0)
    def _():
        o_ref[...] += contrib


def _ffn(x2, gain, w_up, conv_w, conv_b, w_down, *, batch, seq, tm, tf):
    n, d = x2.shape
    f = w_down.shape[0]
    nj = f // tf
    nt = seq // tm
    taps = conv_w.shape[0]
    row = lambda b, t, j: (b * nt + t, 0)
    return pl.pallas_call(
        functools.partial(_ffn_kernel, tm=tm, taps=taps),
        grid=(batch, nt, nj),
        in_specs=[pl.BlockSpec((tm, d), row),
                  pl.BlockSpec((1, d), lambda b, t, j: (0, 0)),
                  pl.BlockSpec((d, tf), lambda b, t, j: (0, j)),
                  pl.BlockSpec((d, tf), lambda b, t, j: (0, nj + j)),
                  pl.BlockSpec((taps, tf), lambda b, t, j: (0, j)),
                  pl.BlockSpec((taps, tf), lambda b, t, j: (0, nj + j)),
                  pl.BlockSpec((1, tf), lambda b, t, j: (0, j)),
                  pl.BlockSpec((1, tf), lambda b, t, j: (0, nj + j)),
                  pl.BlockSpec((tf, d), lambda b, t, j: (j, 0))],
        out_specs=pl.BlockSpec((tm, d), row),
        out_shape=jax.ShapeDtypeStruct((n, d), F32),
        scratch_shapes=[pltpu.VMEM((tm, d), BF16),
                        pltpu.VMEM((nj, SUBLANES, tf), F32),
                        pltpu.VMEM((nj, SUBLANES, tf), F32),
                        pltpu.VMEM((tm + SUBLANES, tf), F32),
                        pltpu.VMEM((tm + SUBLANES, tf), F32)],
        compiler_params=_params(3),
        name="conv_glu_ffn",
    )(x2, gain, w_up, w_up, conv_w, conv_w, conv_b, conv_b, w_down)


def _final_norm_kernel(x_ref, g_ref, o_ref):
    o_ref[...] = _rms(x_ref[...], g_ref[...])


def _final_norm(x2, gain, *, tm):
    n, d = x2.shape
    return pl.pallas_call(
        _final_norm_kernel,
        grid=(n // tm,),
        in_specs=[pl.BlockSpec((tm, d), lambda i: (i, 0)),
                  pl.BlockSpec((1, d), lambda i: (0, 0))],
        out_specs=pl.BlockSpec((tm, d), lambda i: (i, 0)),
        out_shape=jax.ShapeDtypeStruct((n, d), F32),
        compiler_params=_params(1),
        name="final_norm",
    )(x2, gain)


def _tiles(seq, d_ff):
    return dict(
        rope_tm=min(2048, seq),
        in_tm=min(512, seq),
        attn_tq=min(512, seq),
        lru_tc=min(256, seq),
        out_tm=min(1024, seq),
        out_tn=1024,
        ffn_tm=min(512, seq),
        ffn_tf=min(512, d_ff),
        norm_tm=min(1024, seq),
    )


@jax.jit
def _forward(x, positions, attn_norm, w_in, lambda_q1, lambda_k1, lambda_q2, lambda_k2, subln,
             lru_conv_w, lru_conv_b, gate_a_w, gate_a_b, gate_x_w, gate_x_b, lru_lambda, lru_norm,
             w_out, mlp_norm, w_up, ffn_conv_w, ffn_conv_b, w_down, final_norm):
    batch, seq, d = x.shape
    depth = w_in.shape[0]
    lru_w = lru_lambda.shape[1]
    attn_w = d - lru_w
    dv = subln.shape[1]
    dk = lambda_q1.shape[1]
    heads = attn_w // dv
    d_ff = w_down.shape[1]
    assert dv == LANES and 2 * dk == LANES, "one head must fill the 128 lanes"
    assert attn_w == lru_w and w_in.shape[2] == 3 * attn_w + 2 * lru_w
    assert gate_a_w.shape[2] == LANES
    tl = _tiles(seq, d_ff)
    n = batch * seq

    pos_f = positions.astype(F32).reshape(n, 1)
    cos_t, sin_t = _rope_tables(pos_f, dk, tl["rope_tm"])

    vec = lambda p: p.reshape(depth, 1, -1)
    lam_init = jnp.asarray([0.8 - 0.6 * math.exp(-0.3 * l) for l in range(depth)], F32).reshape(depth, 1, 1)
    layers = dict(
        attn_norm=vec(attn_norm), w_in=w_in.astype(BF16),
        lq1=vec(lambda_q1), lk1=vec(lambda_k1), lq2=vec(lambda_q2), lk2=vec(lambda_k2),
        lam_init=lam_init, subln=vec(subln),
        conv_w=lru_conv_w, conv_b=vec(lru_conv_b),
        wg=jnp.concatenate([gate_a_w, gate_x_w], axis=-1).astype(BF16),
        bga=vec(gate_a_b), bgx=vec(gate_x_b), lru_lambda=vec(lru_lambda), lru_norm=vec(lru_norm),
        w_out=w_out.astype(BF16), mlp_norm=vec(mlp_norm), w_up=w_up.astype(BF16),
        ffn_conv_w=ffn_conv_w, ffn_conv_b=vec(ffn_conv_b), w_down=w_down.astype(BF16),
    )

    def layer(x2, p):
        q, k, v, xb, gb = _in_proj(x2, p["attn_norm"], p["w_in"], cos_t, sin_t,
                                   width=attn_w, heads=heads, half=dk // 2, tm=tl["in_tm"])
        attn = _attention(q, k, v, p["lq1"], p["lk1"], p["lq2"], p["lk2"], p["lam_init"], p["subln"],
                          batch=batch, seq=seq, heads=heads, dk=dk, tq=tl["attn_tq"])
        lru = _rglru(xb, gb, p["conv_w"], p["conv_b"], p["wg"], p["bga"], p["bgx"],
                     p["lru_lambda"], p["lru_norm"], batch=batch, seq=seq, tc=tl["lru_tc"])
        x2 = _out_proj(attn, lru, p["w_out"], x2, tm=tl["out_tm"], tn=min(tl["out_tn"], d))
        x2 = _ffn(x2, p["mlp_norm"], p["w_up"], p["ffn_conv_w"], p["ffn_conv_b"], p["w_down"],
                  batch=batch, seq=seq, tm=tl["ffn_tm"], tf=tl["ffn_tf"])
        return x2, None

    x2, _ = lax.scan(layer, x.reshape(n, d), layers)
    return _final_norm(x2, final_norm.reshape(1, d), tm=tl["norm_tm"]).reshape(batch, seq, d)


def kernel(x, positions, attn_norm, w_in, lambda_q1, lambda_k1, lambda_q2, lambda_k2, subln, lru_conv_w, lru_conv_b, gate_a_w, gate_a_b, gate_x_w, gate_x_b, lru_lambda, lru_norm, w_out, mlp_norm, w_up, ffn_conv_w, ffn_conv_b, w_down, final_norm):
    return _forward(x, positions, attn_norm, w_in, lambda_q1, lambda_k1, lambda_q2, lambda_k2, subln,
                    lru_conv_w, lru_conv_b, gate_a_w, gate_a_b, gate_x_w, gate_x_b, lru_lambda, lru_norm,
                    w_out, mlp_norm, w_up, ffn_conv_w, ffn_conv_b, w_down, final_norm)
```

```python
import functools
import math

import jax
import jax.numpy as jnp
from jax import lax
from jax.experimental import pallas as pl
from jax.experimental.pallas import tpu as pltpu

EPS = 1e-6
ROPE_THETA = 10000.0
LRU_C = 8.0
LANES = 128
SUBLANES = 8
VMEM_LIMIT = 56 * 1024 * 1024

F32 = jnp.float32
BF16 = jnp.bfloat16


def _params(n_axes):
    return pltpu.CompilerParams(dimension_semantics=("arbitrary",) * n_axes,
                                vmem_limit_bytes=VMEM_LIMIT)


def _rms(x, gain):
    ms = jnp.mean(x * x, axis=-1, keepdims=True)
    return x * lax.rsqrt(ms + EPS) * gain


def _rope_table_kernel(pos_ref, inv_ref, sign_ref, cos_ref, sin_ref):
    ang = pos_ref[...] * inv_ref[...]
    cos_ref[...] = jnp.cos(ang)
    sin_ref[...] = jnp.sin(ang) * sign_ref[...]


def _rope_tables(pos_f, dk, tm):
    n = pos_f.shape[0]
    half = dk // 2
    inv_freq = 1.0 / (ROPE_THETA ** (jnp.arange(0, dk, 2, dtype=F32) / dk))
    reps = LANES // half
    inv = jnp.tile(inv_freq, reps).reshape(1, LANES)
    sign = jnp.tile(jnp.concatenate([-jnp.ones((half,), F32), jnp.ones((half,), F32)]), reps // 2).reshape(1, LANES)
    return pl.pallas_call(
        _rope_table_kernel,
        grid=(n // tm,),
        in_specs=[pl.BlockSpec((tm, 1), lambda i: (i, 0)),
                  pl.BlockSpec((1, LANES), lambda i: (0, 0)),
                  pl.BlockSpec((1, LANES), lambda i: (0, 0))],
        out_specs=[pl.BlockSpec((tm, LANES), lambda i: (i, 0)),
                   pl.BlockSpec((tm, LANES), lambda i: (i, 0))],
        out_shape=[jax.ShapeDtypeStruct((n, LANES), F32)] * 2,
        compiler_params=_params(1),
        name="rope_tables",
    )(pos_f, inv, sign)


def _in_proj_kernel(x_ref, g_ref, w_ref, cos_ref, sin_ref,
                    q_ref, k_ref, v_ref, xb_ref, gb_ref, h_ref, *, heads, half):
    j = pl.program_id(1)

    @pl.when(j == 0)
    def _():
        h_ref[...] = _rms(x_ref[...], g_ref[...]).astype(BF16)

    def proj():
        return jnp.dot(h_ref[...], w_ref[...], preferred_element_type=F32)

    def rope_store(o_ref):
        acc = proj()
        c = cos_ref[...]
        s = sin_ref[...]
        lane = lax.broadcasted_iota(jnp.int32, c.shape, 1)
        first = (lane % (2 * half)) < half
        for hd in range(heads):
            t = acc[:, hd * LANES:(hd + 1) * LANES]
            partner = jnp.where(first, pltpu.roll(t, LANES - half, 1), pltpu.roll(t, half, 1))
            o_ref[:, hd * LANES:(hd + 1) * LANES] = (t * c + partner * s).astype(o_ref.dtype)

    @pl.when(j == 0)
    def _():
        rope_store(q_ref)

    @pl.when(j == 1)
    def _():
        rope_store(k_ref)

    @pl.when(j == 2)
    def _():
        v_ref[...] = proj().astype(v_ref.dtype)

    @pl.when(j == 3)
    def _():
        xb_ref[...] = proj()

    @pl.when(j == 4)
    def _():
        gb_ref[...] = proj()


def _in_proj(x2, gain, w_in, cos_t, sin_t, *, width, heads, half, tm):
    n, d = x2.shape
    out_spec = pl.BlockSpec((tm, width), lambda i, j: (i, 0))
    return pl.pallas_call(
        functools.partial(_in_proj_kernel, heads=heads, half=half),
        grid=(n // tm, 5),
        in_specs=[pl.BlockSpec((tm, d), lambda i, j: (i, 0)),
                  pl.BlockSpec((1, d), lambda i, j: (0, 0)),
                  pl.BlockSpec((d, width), lambda i, j: (0, j)),
                  pl.BlockSpec((tm, LANES), lambda i, j: (i, 0)),
                  pl.BlockSpec((tm, LANES), lambda i, j: (i, 0))],
        out_specs=[out_spec] * 5,
        out_shape=[jax.ShapeDtypeStruct((n, width), BF16)] * 3 + [jax.ShapeDtypeStruct((n, width), F32)] * 2,
        scratch_shapes=[pltpu.VMEM((tm, d), BF16)],
        compiler_params=_params(2),
        name="in_proj",
    )(x2, gain, w_in, cos_t, sin_t)


def _attn_kernel(q_ref, k_ref, v_ref, lq1_ref, lk1_ref, lq2_ref, lk2_ref, li_ref, sg_ref,
                 o_ref, q2_ref, vt_ref, m_ref, l_ref, acc_ref, *, tq, dk):
    qi = pl.program_id(2)
    nk = vt_ref.shape[0]

    @pl.when(qi == 0)
    def _():
        for c in range(nk):
            vt_ref[c] = v_ref[c * tq:(c + 1) * tq, :].T

    scale = dk ** -0.5
    q = q_ref[...] * jnp.asarray(scale, q_ref.dtype)
    lane = lax.broadcasted_iota(jnp.int32, q.shape, 1)
    lo = lane < dk
    zero = jnp.zeros_like(q)
    q2_ref[0:tq, :] = jnp.where(lo, q, zero)
    q2_ref[tq:2 * tq, :] = jnp.where(lo, zero, q)
    m_ref[...] = jnp.full(m_ref.shape, -jnp.inf, F32)
    l_ref[...] = jnp.zeros(l_ref.shape, F32)
    acc_ref[...] = jnp.zeros(acc_ref.shape, F32)

    def scores(ki):
        kt = k_ref[pl.ds(pl.multiple_of(ki * tq, tq), tq), :]
        return lax.dot_general(kt, q2_ref[...], (((1,), (1,)), ((), ())), preferred_element_type=F32)

    def update(s, ki):
        m_prev = m_ref[...]
        m_new = jnp.maximum(m_prev, jnp.max(s, axis=0, keepdims=True))
        alpha = jnp.exp(m_prev - m_new)
        p = jnp.exp(s - m_new)
        l_ref[...] = alpha * l_ref[...] + jnp.sum(p, axis=0, keepdims=True)
        acc_ref[...] = alpha * acc_ref[...] + jnp.dot(vt_ref[ki], p.astype(BF16), preferred_element_type=F32)
        m_ref[...] = m_new

    def body(ki, carry):
        update(scores(ki), ki)
        return carry

    lax.fori_loop(0, qi, body, 0)

    s = scores(qi)
    key = lax.broadcasted_iota(jnp.int32, s.shape, 0)
    qry = lax.broadcasted_iota(jnp.int32, s.shape, 1)
    qry = jnp.where(qry >= tq, qry - tq, qry)
    update(jnp.where(key <= qry, s, -jnp.inf), qi)

    lam_init = li_ref[...]
    lam = (jnp.exp(jnp.sum(lq1_ref[...] * lk1_ref[...], axis=1, keepdims=True))
           - jnp.exp(jnp.sum(lq2_ref[...] * lk2_ref[...], axis=1, keepdims=True)) + lam_init)
    o_all = acc_ref[...] / l_ref[...]
    o_t = o_all[:, 0:tq] - lam * o_all[:, tq:2 * tq]
    ms = jnp.mean(o_t * o_t, axis=0, keepdims=True)
    o = (o_t * lax.rsqrt(ms + EPS)).T
    o_ref[...] = (o * sg_ref[...] * (1.0 - lam_init)).astype(o_ref.dtype)


def _attention(q, k, v, lq1, lk1, lq2, lk2, lam_init, subln, *, batch, seq, heads, dk, tq):
    n, width = q.shape
    nq = seq // tq
    vec = lambda c: pl.BlockSpec((1, c), lambda b, h, i: (0, 0))
    return pl.pallas_call(
        functools.partial(_attn_kernel, tq=tq, dk=dk),
        grid=(batch, heads, nq),
        in_specs=[pl.BlockSpec((tq, LANES), lambda b, h, i: (b * nq + i, h)),
                  pl.BlockSpec((seq, LANES), lambda b, h, i: (b, h)),
                  pl.BlockSpec((seq, LANES), lambda b, h, i: (b, h)),
                  vec(dk), vec(dk), vec(dk), vec(dk), vec(1), vec(LANES)],
        out_specs=pl.BlockSpec((tq, LANES), lambda b, h, i: (b * nq + i, h)),
        out_shape=jax.ShapeDtypeStruct((n, width), BF16),
        scratch_shapes=[pltpu.VMEM((2 * tq, LANES), BF16),
                        pltpu.VMEM((nq, LANES, tq), BF16),
                        pltpu.VMEM((1, 2 * tq), F32),
                        pltpu.VMEM((1, 2 * tq), F32),
                        pltpu.VMEM((LANES, 2 * tq), F32)],
        compiler_params=_params(3),
        name="diff_attention",
    )(q, k, v, lq1, lk1, lq2, lk2, lam_init, subln)


def _linear_scan(a, b):
    t = a.shape[0]
    row = lax.broadcasted_iota(jnp.int32, a.shape, 0)
    d = 1
    while d < min(t, SUBLANES):
        keep = row >= d
        a_s = pltpu.roll(a, d, 0)
        b_s = pltpu.roll(b, d, 0)
        b = jnp.where(keep, a * b_s + b, b)
        a = jnp.where(keep, a * a_s, a)
        d *= 2
    while d < t:
        b = jnp.concatenate([b[:d], a[d:] * b[:t - d] + b[d:]], axis=0)
        a = jnp.concatenate([a[:d], a[d:] * a[:t - d]], axis=0)
        d *= 2
    return a, b


def _rglru_kernel(xb_ref, gb_ref, cw_ref, cb_ref, wg_ref, bga_ref, bgx_ref, lam_ref, gn_ref,
                  o_ref, xs_ref, h_ref, y_ref, *, tc, blocks, taps):
    t = pl.program_id(1)
    pad = SUBLANES

    @pl.when(t == 0)
    def _():
        xs_ref[0:pad, :] = jnp.zeros((pad, xs_ref.shape[1]), F32)
        h_ref[...] = jnp.zeros(h_ref.shape, F32)

    xs_ref[pad:pad + tc, :] = xb_ref[...]
    ssq = jnp.zeros((tc, 1), F32)
    for hb in range(blocks):
        cs = slice(hb * LANES, (hb + 1) * LANES)
        xc = cb_ref[:, cs] + xs_ref[pad - (taps - 1):pad - (taps - 1) + tc, cs] * cw_ref[0:1, cs]
        for kk in range(1, taps):
            off = pad - (taps - 1) + kk
            xc = xc + xs_ref[off:off + tc, cs] * cw_ref[kk:kk + 1, cs]
        gates = jnp.dot(xc.astype(BF16), wg_ref[hb], preferred_element_type=F32)
        r = jax.nn.sigmoid(gates[:, 0:LANES] + bga_ref[:, cs])
        i = jax.nn.sigmoid(gates[:, LANES:2 * LANES] + bgx_ref[:, cs])
        log_a = -LRU_C * r * jax.nn.softplus(-lam_ref[:, cs])
        a = jnp.exp(log_a)
        mult = jnp.sqrt(jnp.tanh(-log_a) * (1.0 + a * a))
        bterm = mult * (i * xc)
        a_cum, h_loc = _linear_scan(a, bterm)
        h = a_cum * h_ref[0:1, cs] + h_loc
        h_ref[0:1, cs] = h[tc - 1:tc, :]
        y = h * jax.nn.gelu(gb_ref[:, cs])
        y_ref[:, cs] = y
        ssq = ssq + jnp.sum(y * y, axis=1, keepdims=True)

    xs_ref[0:pad, :] = xs_ref[tc:tc + pad, :]
    width = y_ref.shape[1]
    inv = lax.rsqrt(ssq / width + EPS)
    o_ref[...] = (y_ref[...] * inv * gn_ref[...]).astype(o_ref.dtype)


def _rglru(xb, gb, conv_w, conv_b, wg, bga, bgx, lam, gnorm, *, batch, seq, tc):
    n, width = xb.shape
    nt = seq // tc
    blocks = width // LANES
    taps = conv_w.shape[0]
    row = lambda b, t: (b * nt + t, 0)
    fixed2 = lambda b, t: (0, 0)
    vec = pl.BlockSpec((1, width), fixed2)
    return pl.pallas_call(
        functools.partial(_rglru_kernel, tc=tc, blocks=blocks, taps=taps),
        grid=(batch, nt),
        in_specs=[pl.BlockSpec((tc, width), row),
                  pl.BlockSpec((tc, width), row),
                  pl.BlockSpec((taps, width), fixed2),
                  vec,
                  pl.BlockSpec((blocks, LANES, 2 * LANES), lambda b, t: (0, 0, 0)),
                  vec, vec, vec, vec],
        out_specs=pl.BlockSpec((tc, width), row),
        out_shape=jax.ShapeDtypeStruct((n, width), BF16),
        scratch_shapes=[pltpu.VMEM((tc + SUBLANES, width), F32),
                        pltpu.VMEM((SUBLANES, width), F32),
                        pltpu.VMEM((tc, width), F32)],
        compiler_params=_params(2),
        name="rglru",
    )(xb, gb, conv_w, conv_b, wg, bga, bgx, lam, gnorm)


def _out_proj_kernel(a_ref, r_ref, wa_ref, wr_ref, x_ref, o_ref):
    o_ref[...] = (x_ref[...]
                  + jnp.dot(a_ref[...], wa_ref[...], preferred_element_type=F32)
                  + jnp.dot(r_ref[...], wr_ref[...], preferred_element_type=F32))


def _out_proj(attn, lru, w_out, x2, *, tm, tn):
    n, d = x2.shape
    wa = attn.shape[1]
    wr = lru.shape[1]
    assert wa == wr, "the two mixer groups must be equally wide"
    return pl.pallas_call(
        _out_proj_kernel,
        grid=(n // tm, d // tn),
        in_specs=[pl.BlockSpec((tm, wa), lambda i, j: (i, 0)),
                  pl.BlockSpec((tm, wr), lambda i, j: (i, 0)),
                  pl.BlockSpec((wa, tn), lambda i, j: (0, j)),
                  pl.BlockSpec((wr, tn), lambda i, j: (1, j)),
                  pl.BlockSpec((tm, tn), lambda i, j: (i, j))],
        out_specs=pl.BlockSpec((tm, tn), lambda i, j: (i, j)),
        out_shape=jax.ShapeDtypeStruct((n, d), F32),
        compiler_params=_params(2),
        name="out_proj",
    )(attn, lru, w_out, w_out, x2)


def _ffn_kernel(x_ref, g_ref, wg_ref, wv_ref, cwg_ref, cwv_ref, cbg_ref, cbv_ref, wd_ref,
                o_ref, h_ref, tg_ref, tv_ref, bg_ref, bv_ref, *, tm, taps):
    t = pl.program_id(1)
    j = pl.program_id(2)
    pad = SUBLANES

    @pl.when(j == 0)
    def _():
        h_ref[...] = _rms(x_ref[...], g_ref[...]).astype(BF16)

    def conv(w_ref, cw_ref, cb_ref, tail_ref, buf_ref):
        u = jnp.dot(h_ref[...], w_ref[...], preferred_element_type=F32)

        @pl.when(t == 0)
        def _():
            buf_ref[0:pad, :] = jnp.zeros((pad, buf_ref.shape[1]), F32)

        @pl.when(t > 0)
        def _():
            buf_ref[0:pad, :] = tail_ref[j]

        buf_ref[pad:pad + tm, :] = u
        tail_ref[j] = u[tm - pad:tm, :]
        y = cb_ref[...] + buf_ref[pad - (taps - 1):pad - (taps - 1) + tm, :] * cw_ref[0:1, :]
        for kk in range(1, taps - 1):
            off = pad - (taps - 1) + kk
            y = y + buf_ref[off:off + tm, :] * cw_ref[kk:kk + 1, :]
        return y + u * cw_ref[taps - 1:taps, :]

    yg = conv(wg_ref, cwg_ref, cbg_ref, tg_ref, bg_ref)
    yv = conv(wv_ref, cwv_ref, cbv_ref, tv_ref, bv_ref)
    act = (jax.nn.gelu(yg) * yv).astype(BF16)
    contrib = jnp.dot(act, wd_ref[...], preferred_element_type=F32)

    @pl.when(j == 0)
    def _():
        o_ref[...] = x_ref[...] + contrib

    @pl.when(j > 0)
    def _():
        o_ref[...] += contrib


def _ffn(x2, gain, w_up, conv_w, conv_b, w_down, *, batch, seq, tm, tf):
    n, d = x2.shape
    f = w_down.shape[0]
    nj = f // tf
    nt = seq // tm
    taps = conv_w.shape[0]
    row = lambda b, t, j: (b * nt + t, 0)
    return pl.pallas_call(
        functools.partial(_ffn_kernel, tm=tm, taps=taps),
        grid=(batch, nt, nj),
        in_specs=[pl.BlockSpec((tm, d), row),
                  pl.BlockSpec((1, d), lambda b, t, j: (0, 0)),
                  pl.BlockSpec((d, tf), lambda b, t, j: (0, j)),
                  pl.BlockSpec((d, tf), lambda b, t, j: (0, nj + j)),
                  pl.BlockSpec((taps, tf), lambda b, t, j: (0, j)),
                  pl.BlockSpec((taps, tf), lambda b, t, j: (0, nj + j)),
                  pl.BlockSpec((1, tf), lambda b, t, j: (0, j)),
                  pl.BlockSpec((1, tf), lambda b, t, j: (0, nj + j)),
                  pl.BlockSpec((tf, d), lambda b, t, j: (j, 0))],
        out_specs=pl.BlockSpec((tm, d), row),
        out_shape=jax.ShapeDtypeStruct((n, d), F32),
        scratch_shapes=[pltpu.VMEM((tm, d), BF16),
                        pltpu.VMEM((nj, SUBLANES, tf), F32),
                        pltpu.VMEM((nj, SUBLANES, tf), F32),
                        pltpu.VMEM((tm + SUBLANES, tf), F32),
                        pltpu.VMEM((tm + SUBLANES, tf), F32)],
        compiler_params=_params(3),
        name="conv_glu_ffn",
    )(x2, gain, w_up, w_up, conv_w, conv_w, conv_b, conv_b, w_down)


def _final_norm_kernel(x_ref, g_ref, o_ref):
    o_ref[...] = _rms(x_ref[...], g_ref[...])


def _final_norm(x2, gain, *, tm):
    n, d = x2.shape
    return pl.pallas_call(
        _final_norm_kernel,
        grid=(n // tm,),
        in_specs=[pl.BlockSpec((tm, d), lambda i: (i, 0)),
                  pl.BlockSpec((1, d), lambda i: (0, 0))],
        out_specs=pl.BlockSpec((tm, d), lambda i: (i, 0)),
        out_shape=jax.ShapeDtypeStruct((n, d), F32),
        compiler_params=_params(1),
        name="final_norm",
    )(x2, gain)


def _tiles(seq, d_ff):
    return dict(
        rope_tm=min(2048, seq),
        in_tm=min(512, seq),
        attn_tq=min(512, seq),
        lru_tc=min(256, seq),
        out_tm=min(1024, seq),
        out_tn=1024,
        ffn_tm=min(512, seq),
        ffn_tf=min(512, d_ff),
        norm_tm=min(1024, seq),
    )


@jax.jit
def _forward(x, positions, attn_norm, w_in, lambda_q1, lambda_k1, lambda_q2, lambda_k2, subln,
             lru_conv_w, lru_conv_b, gate_a_w, gate_a_b, gate_x_w, gate_x_b, lru_lambda, lru_norm,
             w_out, mlp_norm, w_up, ffn_conv_w, ffn_conv_b, w_down, final_norm):
    batch, seq, d = x.shape
    depth = w_in.shape[0]
    lru_w = lru_lambda.shape[1]
    attn_w = d - lru_w
    dv = subln.shape[1]
    dk = lambda_q1.shape[1]
    heads = attn_w // dv
    d_ff = w_down.shape[1]
    assert dv == LANES and 2 * dk == LANES, "one head must fill the 128 lanes"
    assert attn_w == lru_w and w_in.shape[2] == 3 * attn_w + 2 * lru_w
    assert gate_a_w.shape[2] == LANES
    tl = _tiles(seq, d_ff)
    n = batch * seq

    pos_f = positions.astype(F32).reshape(n, 1)
    cos_t, sin_t = _rope_tables(pos_f, dk, tl["rope_tm"])

    vec = lambda p: p.reshape(depth, 1, -1)
    lam_init = jnp.asarray([0.8 - 0.6 * math.exp(-0.3 * l) for l in range(depth)], F32).reshape(depth, 1, 1)
    layers = dict(
        attn_norm=vec(attn_norm), w_in=w_in.astype(BF16),
        lq1=vec(lambda_q1), lk1=vec(lambda_k1), lq2=vec(lambda_q2), lk2=vec(lambda_k2),
        lam_init=lam_init, subln=vec(subln),
        conv_w=lru_conv_w, conv_b=vec(lru_conv_b),
        wg=jnp.concatenate([gate_a_w, gate_x_w], axis=-1).astype(BF16),
        bga=vec(gate_a_b), bgx=vec(gate_x_b), lru_lambda=vec(lru_lambda), lru_norm=vec(lru_norm),
        w_out=w_out.astype(BF16), mlp_norm=vec(mlp_norm), w_up=w_up.astype(BF16),
        ffn_conv_w=ffn_conv_w, ffn_conv_b=vec(ffn_conv_b), w_down=w_down.astype(BF16),
    )

    def layer(x2, p):
        q, k, v, xb, gb = _in_proj(x2, p["attn_norm"], p["w_in"], cos_t, sin_t,
                                   width=attn_w, heads=heads, half=dk // 2, tm=tl["in_tm"])
        attn = _attention(q, k, v, p["lq1"], p["lk1"], p["lq2"], p["lk2"], p["lam_init"], p["subln"],
                          batch=batch, seq=seq, heads=heads, dk=dk, tq=tl["attn_tq"])
        lru = _rglru(xb, gb, p["conv_w"], p["conv_b"], p["wg"], p["bga"], p["bgx"],
                     p["lru_lambda"], p["lru_norm"], batch=batch, seq=seq, tc=tl["lru_tc"])
        x2 = _out_proj(attn, lru, p["w_out"], x2, tm=tl["out_tm"], tn=min(tl["out_tn"], d))
        x2 = _ffn(x2, p["mlp_norm"], p["w_up"], p["ffn_conv_w"], p["ffn_conv_b"], p["w_down"],
                  batch=batch, seq=seq, tm=tl["ffn_tm"], tf=tl["ffn_tf"])
        return x2, None

    x2, _ = lax.scan(layer, x.reshape(n, d), layers)
    return _final_norm(x2, final_norm.reshape(1, d), tm=tl["norm_tm"]).reshape(batch, seq, d)


def kernel(x, positions, attn_norm, w_in, lambda_q1, lambda_k1, lambda_q2, lambda_k2, subln, lru_conv_w, lru_conv_b, gate_a_w, gate_a_b, gate_x_w, gate_x_b, lru_lambda, lru_norm, w_out, mlp_norm, w_up, ffn_conv_w, ffn_conv_b, w_down, final_norm):
    return _forward(x, positions, attn_norm, w_in, lambda_q1, lambda_k1, lambda_q2, lambda_k2, subln,
                    lru_conv_w, lru_conv_b, gate_a_w, gate_a_b, gate_x_w, gate_x_b, lru_lambda, lru_norm,
                    w_out, mlp_norm, w_up, ffn_conv_w, ffn_conv_b, w_down, final_norm)
```
